```python
import math
import jax, jax.numpy as jnp
from jax import lax
import numpy as np

D_MODEL = 1024
BATCH = 2
SEQ = 8192
DEPTH = 1

CHUNK = 64
Q_BLOCK = 128
HEAD_DIM = 64
DA_HEADS = 4
DA_VDIM = 2 * HEAD_DIM
DA_QK_WIDTH = DA_HEADS * 2 * HEAD_DIM
DA_WIDTH = DA_HEADS * DA_VDIM
SB_HEADS = 8
SB_WIDTH = SB_HEADS * HEAD_DIM
MIX_WIDTH = DA_WIDTH + SB_WIDTH
IN_WIDTH = 2 * DA_QK_WIDTH + DA_WIDTH + 3 * SB_WIDTH
ROPE_THETA = 500000.0
ROPE_DIM = HEAD_DIM // 4
PEER_HEADS = 8
PEER_QDIM = 256
PEER_HALF = PEER_QDIM // 2
N_KEYS = 128
N_EXPERTS = N_KEYS * N_KEYS
PEER_TOPK = 16
PEER_TOKEN_BLOCK = 128
EPS = 1e-6

kernel_name = "hybrid_diffattn_stickbreaking_peer"


def rmsnorm(x, g):
    xf = x.astype(jnp.float32)
    return xf * lax.rsqrt(jnp.mean(xf * xf, axis=-1, keepdims=True) + EPS) * g.astype(jnp.float32)


def rope_tables(seq):
    pos = jnp.arange(seq, dtype=jnp.float32)
    inv_freq = 1.0 / (ROPE_THETA ** (jnp.arange(0, ROPE_DIM, 2, dtype=jnp.float32) / ROPE_DIM))
    ang = pos[:, None] * inv_freq[None, :]
    return jnp.cos(ang), jnp.sin(ang)


def partial_rope(t, cos, sin):
    half = ROPE_DIM // 2
    t1 = t[..., :half]
    t2 = t[..., half:ROPE_DIM]
    return jnp.concatenate([t1 * cos - t2 * sin, t2 * cos + t1 * sin, t[..., ROPE_DIM:]], axis=-1)


def diff_attention(q, k, v, lam, sub_g, lambda_init):
    seq = q.shape[3]
    scale = HEAD_DIM ** -0.5
    outs = []
    for i in range(seq // Q_BLOCK):
        q0 = i * Q_BLOCK
        kend = q0 + Q_BLOCK
        s = jnp.einsum('bhcqd,bhckd->bhcqk', q[:, :, :, q0:kend], k[:, :, :, :kend]) * scale
        qpos = jnp.arange(q0, kend)
        kpos = jnp.arange(kend)
        mask = (kpos[None, :] // CHUNK) <= (qpos[:, None] // CHUNK)
        p = jax.nn.softmax(jnp.where(mask, s, -jnp.inf), axis=-1)
        w = p[:, :, 0] - lam * p[:, :, 1]
        outs.append(jnp.einsum('bhqk,bhkd->bhqd', w, v[:, :, :kend]))
    o = jnp.concatenate(outs, axis=2)
    return rmsnorm(o, sub_g) * (1.0 - lambda_init)


def stick_breaking_attention(q, k, v):
    seq = q.shape[2]
    scale = HEAD_DIM ** -0.5
    outs = []
    for i in range(seq // Q_BLOCK):
        q0 = i * Q_BLOCK
        kend = q0 + Q_BLOCK
        z = jnp.einsum('bhqd,bhkd->bhqk', q[:, :, q0:kend], k[:, :, :kend]) * scale
        qpos = jnp.arange(q0, kend)
        kpos = jnp.arange(kend)
        mask = kpos[None, :] < qpos[:, None]
        log_keep = jnp.where(mask, jax.nn.log_sigmoid(-z), 0.0)
        suffix = lax.cumsum(log_keep, axis=z.ndim - 1, reverse=True)
        log_a = jax.nn.log_sigmoid(z) + (suffix - log_keep)
        a = jnp.where(mask, jnp.exp(log_a), 0.0)
        outs.append(jnp.einsum('bhqk,bhkd->bhqd', a, v[:, :, :kend]))
    return jnp.concatenate(outs, axis=2)


def peer(xn, wq, keys, u, v):
    b, s, d = xn.shape
    t = b * s
    xt = xn.reshape(t, d)
    q = (xt @ wq.astype(jnp.float32)).reshape(t, PEER_HEADS, 2, PEER_HALF)
    sc = jnp.einsum('thcd,hcnd->thcn', q, keys.astype(jnp.float32))
    v1, i1 = lax.top_k(sc[:, :, 0], PEER_TOPK)
    v2, i2 = lax.top_k(sc[:, :, 1], PEER_TOPK)
    cand = (v1[..., :, None] + v2[..., None, :]).reshape(t, PEER_HEADS, PEER_TOPK * PEER_TOPK)
    best, ci = lax.top_k(cand, PEER_TOPK)
    e = (jnp.take_along_axis(i1, ci // PEER_TOPK, axis=-1) * N_KEYS
         + jnp.take_along_axis(i2, ci % PEER_TOPK, axis=-1))
    g = jax.nn.softmax(best, axis=-1)
    nb = t // PEER_TOKEN_BLOCK
    xb = xt.reshape(nb, PEER_TOKEN_BLOCK, d)
    eb = e.reshape(nb, PEER_TOKEN_BLOCK, PEER_HEADS * PEER_TOPK)
    gb = g.reshape(nb, PEER_TOKEN_BLOCK, PEER_HEADS * PEER_TOPK)

    def block(args):
        xx, ee, gg = args
        hid = jnp.einsum('td,tkd->tk', xx, u[ee].astype(jnp.float32))
        act = jax.nn.gelu(hid, approximate=False) * gg
        return jnp.einsum('tk,tkd->td', act, v[ee].astype(jnp.float32))

    out = lax.map(block, (xb, eb, gb))
    return out.reshape(b, s, d)


def setup_inputs(seed: int = 0) -> dict:
    key = jax.random.key(seed)
    ks = jax.random.split(key, 12)
    f32 = jnp.float32
    nrm = lambda k, shp, sc: jax.random.normal(k, shp, f32) * sc
    return {
        "x": nrm(ks[0], (BATCH, SEQ, D_MODEL), 1.0),
        "norm1_g": 1.0 + nrm(ks[1], (DEPTH, D_MODEL), 0.02),
        "w_in": nrm(ks[2], (DEPTH, D_MODEL, IN_WIDTH), D_MODEL ** -0.5),
        "da_qk_norm_g": 1.0 + nrm(ks[3], (DEPTH, 2, HEAD_DIM), 0.02),
        "da_lambda": nrm(ks[4], (DEPTH, 4, HEAD_DIM), 0.1),
        "da_subln_g": 1.0 + nrm(ks[5], (DEPTH, DA_VDIM), 0.02),
        "w_out": nrm(ks[6], (DEPTH, MIX_WIDTH, D_MODEL), MIX_WIDTH ** -0.5),
        "norm2_g": 1.0 + nrm(ks[7], (DEPTH, D_MODEL), 0.02),
        "peer_wq": nrm(ks[8], (DEPTH, D_MODEL, PEER_HEADS * PEER_QDIM), D_MODEL ** -0.5),
        "peer_keys": nrm(ks[9], (DEPTH, PEER_HEADS, 2, N_KEYS, PEER_HALF), PEER_HALF ** -0.5),
        "peer_u": nrm(ks[10], (DEPTH, N_EXPERTS, D_MODEL), D_MODEL ** -0.5),
        "peer_v": nrm(ks[11], (DEPTH, N_EXPERTS, D_MODEL), PEER_HEADS ** -0.5),
    }


def reference(x, norm1_g, w_in, da_qk_norm_g, da_lambda, da_subln_g, w_out, norm2_g,
              peer_wq, peer_keys, peer_u, peer_v):
    b, s, _ = x.shape
    cos, sin = rope_tables(s)
    splits = [DA_QK_WIDTH, 2 * DA_QK_WIDTH, 2 * DA_QK_WIDTH + DA_WIDTH,
              2 * DA_QK_WIDTH + DA_WIDTH + SB_WIDTH, 2 * DA_QK_WIDTH + DA_WIDTH + 2 * SB_WIDTH]
    h_res = x.astype(jnp.float32)
    for l in range(DEPTH):
        lambda_init = 0.8 - 0.6 * math.exp(-0.3 * l)
        hn = rmsnorm(h_res, norm1_g[l])
        proj = hn @ w_in[l].astype(jnp.float32)
        qa, ka, va, qb, kb, vb = jnp.split(proj, splits, axis=-1)
        qa = qa.reshape(b, s, DA_HEADS, 2, HEAD_DIM).transpose(0, 2, 3, 1, 4)
        ka = ka.reshape(b, s, DA_HEADS, 2, HEAD_DIM).transpose(0, 2, 3, 1, 4)
        va = va.reshape(b, s, DA_HEADS, DA_VDIM).transpose(0, 2, 1, 3)
        qa = partial_rope(rmsnorm(qa, da_qk_norm_g[l, 0]), cos, sin)
        ka = partial_rope(rmsnorm(ka, da_qk_norm_g[l, 1]), cos, sin)
        lp = da_lambda[l].astype(jnp.float32)
        lam = jnp.exp(jnp.sum(lp[0] * lp[1])) - jnp.exp(jnp.sum(lp[2] * lp[3])) + lambda_init
        oa = diff_attention(qa, ka, va, lam, da_subln_g[l], lambda_init)
        qb = qb.reshape(b, s, SB_HEADS, HEAD_DIM).transpose(0, 2, 1, 3)
        kb = kb.reshape(b, s, SB_HEADS, HEAD_DIM).transpose(0, 2, 1, 3)
        vb = vb.reshape(b, s, SB_HEADS, HEAD_DIM).transpose(0, 2, 1, 3)
        ob = stick_breaking_attention(qb, kb, vb)
        mix = jnp.concatenate([oa.transpose(0, 2, 1, 3).reshape(b, s, DA_WIDTH),
                               ob.transpose(0, 2, 1, 3).reshape(b, s, SB_WIDTH)], axis=-1)
        h_res = h_res + mix @ w_out[l].astype(jnp.float32)
        h_res = h_res + peer(rmsnorm(h_res, norm2_g[l]), peer_wq[l], peer_keys[l], peer_u[l], peer_v[l])
    return h_res.astype(x.dtype)
```

```python
import functools
import math

import jax
import jax.numpy as jnp
from jax import lax
from jax.experimental import pallas as pl
from jax.experimental.pallas import tpu as pltpu

F32 = jnp.float32
BF16 = jnp.bfloat16

D_MODEL = 1024
CHUNK = 64
HEAD_DIM = 64
DA_HEADS = 4
SB_HEADS = 8
HALF_WIDTH = 512
IN_WIDTH = 6 * HALF_WIDTH
ROPE_THETA = 500000.0
ROPE_DIM = HEAD_DIM // 4
PEER_HEADS = 8
N_KEYS = 128
N_EXPERTS = N_KEYS * N_KEYS
PEER_TOPK = 16
EPS = 1e-6
LANES = 128
NEG_BIG = -1e30

VMEM_LIMIT = 56 * 1024 * 1024


def _pick(n, prefs):
    for p in prefs:
        if n % p == 0:
            return p
    raise ValueError(f"no tile for {n}")


def _nt_dot(a, b):
    return lax.dot_general(a, b, (((1,), (1,)), ((), ())), preferred_element_type=F32)


def _inproj_kernel(x_ref, g1_ref, w_ref, gqk_ref, gsum_ref, cos_ref, sa_ref, sb_ref, o_ref):
    x = x_ref[...]
    ms = jnp.mean(x * x, axis=-1, keepdims=True)
    hn = (x * lax.rsqrt(ms + EPS) * g1_ref[...]).astype(BF16)
    cos = cos_ref[...]
    sa = sa_ref[...]
    sb = sb_ref[...]
    gsum = gsum_ref[...]

    def section(sec):
        return jnp.dot(hn, w_ref[:, sec * HALF_WIDTH:(sec + 1) * HALF_WIDTH], preferred_element_type=F32)

    def qk_norm_rope(t, gain, scale):
        sq = t * t
        hi = sq.astype(BF16)
        lo = (sq - hi.astype(F32)).astype(BF16)
        ss = (jnp.dot(hi, gsum, preferred_element_type=F32)
              + jnp.dot(lo, gsum, preferred_element_type=F32))
        tn = t * lax.rsqrt(ss * (1.0 / HEAD_DIM) + EPS) * gain
        outs = []
        for blk in range(HALF_WIDTH // LANES):
            tb = tn[:, blk * LANES:(blk + 1) * LANES]
            r = (tb * cos + pltpu.roll(tb, LANES - ROPE_DIM // 2, 1) * sa
                 + pltpu.roll(tb, ROPE_DIM // 2, 1) * sb)
            outs.append((r * scale).astype(BF16))
        return outs

    scale = HEAD_DIM ** -0.5
    nblk = HALF_WIDTH // LANES
    qa = qk_norm_rope(section(0), gqk_ref[0:1, :], scale)
    for blk in range(nblk):
        o_ref[blk] = qa[blk]
    ka = qk_norm_rope(section(1), gqk_ref[1:2, :], 1.0)
    for blk in range(nblk):
        o_ref[nblk + blk] = ka[blk]
    for sec, sc in ((2, 1.0), (3, scale), (4, 1.0), (5, 1.0)):
        t = section(sec)
        for blk in range(nblk):
            o_ref[sec * nblk + blk] = (t[:, blk * LANES:(blk + 1) * LANES] * sc).astype(BF16)


def _inproj(x2, g1, w_bf, gqk, gsum, cosf, saf, sbf, seq):
    t = x2.shape[0]
    tb = _pick(seq, (512, 256, 128))
    ns = seq // tb
    nslab = IN_WIDTH // LANES
    return pl.pallas_call(
        _inproj_kernel,
        grid=(t // tb,),
        in_specs=[
            pl.BlockSpec((tb, D_MODEL), lambda i: (i, 0)),
            pl.BlockSpec((1, D_MODEL), lambda i: (0, 0)),
            pl.BlockSpec((D_MODEL, IN_WIDTH), lambda i: (0, 0)),
            pl.BlockSpec((2, HALF_WIDTH), lambda i: (0, 0)),
            pl.BlockSpec((HALF_WIDTH, HALF_WIDTH), lambda i: (0, 0)),
            pl.BlockSpec((tb, LANES), lambda i: (i % ns, 0)),
            pl.BlockSpec((tb, LANES), lambda i: (i % ns, 0)),
            pl.BlockSpec((tb, LANES), lambda i: (i % ns, 0)),
        ],
        out_specs=pl.BlockSpec((nslab, tb, LANES), lambda i: (0, i, 0)),
        out_shape=jax.ShapeDtypeStruct((nslab, t, LANES), BF16),
        compiler_params=pltpu.CompilerParams(
            dimension_semantics=("arbitrary",), vmem_limit_bytes=VMEM_LIMIT),
        name="inproj",
    )(x2, g1, w_bf, gqk, gsum, cosf, saf, sbf)


def _diffattn_kernel(q_ref, k_ref, v_ref, lam_ref, g_ref, o_ref,
                     m1_ref, l1_ref, a1_ref, m2_ref, l2_ref, a2_ref, *, tq, lambda_init):
    i = pl.program_id(2)
    q = q_ref[...]
    lane = lax.broadcasted_iota(jnp.int32, (tq, LANES), 1)
    first = lane < HEAD_DIM
    stats = ((m1_ref, l1_ref, a1_ref), (m2_ref, l2_ref, a2_ref))

    def scores(j):
        off = pl.multiple_of(j * tq, tq)
        kt = k_ref[pl.ds(off, tq), :]
        vt = v_ref[pl.ds(off, tq), :]
        zero = jnp.zeros_like(kt)
        s1 = _nt_dot(q, jnp.where(first, kt, zero))
        s2 = _nt_dot(q, jnp.where(first, zero, kt))
        return (s1, s2), vt

    (s1, s2), vt = scores(i)
    qc = lax.broadcasted_iota(jnp.int32, (tq, tq), 0) // CHUNK
    kc = lax.broadcasted_iota(jnp.int32, (tq, tq), 1) // CHUNK
    allowed = kc <= qc
    for s, (m_ref, l_ref, a_ref) in zip((s1, s2), stats):
        s = jnp.where(allowed, s, NEG_BIG)
        m = jnp.max(s, axis=-1, keepdims=True)
        p = jnp.exp(s - m)
        m_ref[...] = jnp.broadcast_to(m, (tq, LANES))
        l_ref[...] = jnp.broadcast_to(jnp.sum(p, axis=-1, keepdims=True), (tq, LANES))
        a_ref[...] = jnp.dot(p.astype(BF16), vt, preferred_element_type=F32)

    def body(j, carry):
        (s1, s2), vt = scores(j)
        for s, (m_ref, l_ref, a_ref) in zip((s1, s2), stats):
            m_old = m_ref[...][:, :1]
            m_new = jnp.maximum(m_old, jnp.max(s, axis=-1, keepdims=True))
            alpha = jnp.exp(m_old - m_new)
            p = jnp.exp(s - m_new)
            m_ref[...] = jnp.broadcast_to(m_new, (tq, LANES))
            l_ref[...] = alpha * l_ref[...] + jnp.sum(p, axis=-1, keepdims=True)
            a_ref[...] = alpha * a_ref[...] + jnp.dot(p.astype(BF16), vt, preferred_element_type=F32)
        return carry

    lax.fori_loop(0, i, body, 0)

    lp = lam_ref[...]
    lam = (jnp.exp(jnp.sum(lp[0:1] * lp[1:2], axis=-1, keepdims=True))
           - jnp.exp(jnp.sum(lp[2:3] * lp[3:4], axis=-1, keepdims=True)) + lambda_init)
    o = a1_ref[...] / l1_ref[...] - lam * (a2_ref[...] / l2_ref[...])
    ms = jnp.mean(o * o, axis=-1, keepdims=True)
    o = o * lax.rsqrt(ms + EPS) * g_ref[...] * (1.0 - lambda_init)
    o_ref[...] = o.astype(BF16)


def _diffattn(proj, lam_p, subln_g, batch, seq, lambda_init):
    t = batch * seq
    tq = _pick(seq, (256, 128))
    nq = seq // tq
    nblk = HALF_WIDTH // LANES
    kern = functools.partial(_diffattn_kernel, tq=tq, lambda_init=lambda_init)
    stat = pltpu.VMEM((tq, LANES), F32)
    return pl.pallas_call(
        kern,
        grid=(batch, DA_HEADS, nq),
        in_specs=[
            pl.BlockSpec((None, tq, LANES), lambda b, h, i: (h, b * nq + i, 0)),
            pl.BlockSpec((None, seq, LANES), lambda b, h, i: (nblk + h, b, 0)),
            pl.BlockSpec((None, seq, LANES), lambda b, h, i: (2 * nblk + h, b, 0)),
            pl.BlockSpec((4, HEAD_DIM), lambda b, h, i: (0, 0)),
            pl.BlockSpec((1, LANES), lambda b, h, i: (0, 0)),
        ],
        out_specs=pl.BlockSpec((tq, LANES), lambda b, h, i: (b * nq + i, h)),
        out_shape=jax.ShapeDtypeStruct((t, HALF_WIDTH), BF16),
        scratch_shapes=[stat, stat, stat, stat, stat, stat],
        compiler_params=pltpu.CompilerParams(
            dimension_semantics=("arbitrary", "arbitrary", "arbitrary"), vmem_limit_bytes=VMEM_LIMIT),
        name="diffattn",
    )(proj, proj, proj, lam_p, subln_g)


def _sb_kernel(q_ref, k_ref, v_ref, o_ref, c0_ref, c1_ref, a0_ref, a1_ref, *, tq):
    i = pl.program_id(2)
    q = q_ref[...]
    lane = lax.broadcasted_iota(jnp.int32, (tq, LANES), 1)
    first = lane < HEAD_DIM
    jj = lax.broadcasted_iota(jnp.int32, (tq, tq), 0)
    ss = lax.broadcasted_iota(jnp.int32, (tq, tq), 1)
    tri = jnp.where(jj >= ss, 1.0, 0.0).astype(BF16)
    strict = ss < jj
    state = ((c0_ref, a0_ref), (c1_ref, a1_ref))

    def tile(j, diag):
        off = pl.multiple_of(j * tq, tq)
        kt = k_ref[pl.ds(off, tq), :]
        vt = v_ref[pl.ds(off, tq), :]
        zero = jnp.zeros_like(kt)
        for hh, (c_ref, a_ref) in enumerate(state):
            kh = jnp.where(first, kt, zero) if hh == 0 else jnp.where(first, zero, kt)
            z = _nt_dot(q, kh)
            sp = jnp.maximum(z, 0.0) + jnp.log(1.0 + jnp.exp(-jnp.abs(z)))
            if diag:
                sp = jnp.where(strict, sp, 0.0)
            csum = jnp.dot(sp.astype(BF16), tri, preferred_element_type=F32)
            if diag:
                a = jnp.where(strict, jnp.exp(z - csum), 0.0)
                a_ref[...] = jnp.dot(a.astype(BF16), vt, preferred_element_type=F32)
                c_ref[...] = jnp.broadcast_to(csum[:, :1], (tq, LANES))
            else:
                carry = c_ref[...][:, :1]
                a = jnp.exp(z - csum - carry)
                a_ref[...] += jnp.dot(a.astype(BF16), vt, preferred_element_type=F32)
                c_ref[...] = jnp.broadcast_to(carry + csum[:, :1], (tq, LANES))

    tile(i, True)

    def body(n, carry):
        tile(i - 1 - n, False)
        return carry

    lax.fori_loop(0, i, body, 0)
    o_ref[...] = jnp.where(first, a0_ref[...], a1_ref[...]).astype(BF16)


def _stickbreak(proj, batch, seq):
    t = batch * seq
    tq = _pick(seq, (256, 128))
    nq = seq // tq
    nblk = HALF_WIDTH // LANES
    kern = functools.partial(_sb_kernel, tq=tq)
    stat = pltpu.VMEM((tq, LANES), F32)
    return pl.pallas_call(
        kern,
        grid=(batch, SB_HEADS // 2, nq),
        in_specs=[
            pl.BlockSpec((None, tq, LANES), lambda b, h, i: (3 * nblk + h, b * nq + i, 0)),
            pl.BlockSpec((None, seq, LANES), lambda b, h, i: (4 * nblk + h, b, 0)),
            pl.BlockSpec((None, seq, LANES), lambda b, h, i: (5 * nblk + h, b, 0)),
        ],
        out_specs=pl.BlockSpec((tq, LANES), lambda b, h, i: (b * nq + i, h)),
        out_shape=jax.ShapeDtypeStruct((t, HALF_WIDTH), BF16),
        scratch_shapes=[stat, stat, stat, stat],
        compiler_params=pltpu.CompilerParams(
            dimension_semantics=("arbitrary", "arbitrary", "arbitrary"), vmem_limit_bytes=VMEM_LIMIT),
        name="stickbreak",
    )(proj, proj, proj)


def _outproj_kernel(x_ref, oa_ref, ob_ref, wo_ref, g2_ref, wqt_ref, keys_ref,
                    h1_ref, xnt_ref, sc_ref):
    h1 = (x_ref[...]
          + jnp.dot(oa_ref[...], wo_ref[0:HALF_WIDTH, :], preferred_element_type=F32)
          + jnp.dot(ob_ref[...], wo_ref[HALF_WIDTH:, :], preferred_element_type=F32))
    h1_ref[...] = h1
    ms = jnp.mean(h1 * h1, axis=-1, keepdims=True)
    xn = h1 * lax.rsqrt(ms + EPS) * g2_ref[...]
    xnt = xn.T.astype(BF16)
    xnt_ref[...] = xnt
    pqt = jnp.dot(wqt_ref[...], xnt, preferred_element_type=F32)
    for hc in range(2 * PEER_HEADS):
        qh = pqt[hc * N_KEYS:(hc + 1) * N_KEYS, :].astype(BF16)
        sc_ref[hc] = jnp.dot(keys_ref[hc], qh, preferred_element_type=F32)


def _outproj(x2, oa, ob, wo_bf, g2, wqt_bf, keys_bf):
    t = x2.shape[0]
    tb = _pick(t, (512, 256, 128))
    nhc = 2 * PEER_HEADS
    return pl.pallas_call(
        _outproj_kernel,
        grid=(t // tb,),
        in_specs=[
            pl.BlockSpec((tb, D_MODEL), lambda i: (i, 0)),
            pl.BlockSpec((tb, HALF_WIDTH), lambda i: (i, 0)),
            pl.BlockSpec((tb, HALF_WIDTH), lambda i: (i, 0)),
            pl.BlockSpec((D_MODEL, D_MODEL), lambda i: (0, 0)),
            pl.BlockSpec((1, D_MODEL), lambda i: (0, 0)),
            pl.BlockSpec((nhc * N_KEYS, D_MODEL), lambda i: (0, 0)),
            pl.BlockSpec((nhc, N_KEYS, N_KEYS), lambda i: (0, 0, 0)),
        ],
        out_specs=[
            pl.BlockSpec((tb, D_MODEL), lambda i: (i, 0)),
            pl.BlockSpec((D_MODEL, tb), lambda i: (0, i)),
            pl.BlockSpec((nhc, N_KEYS, tb), lambda i: (0, 0, i)),
        ],
        out_shape=[
            jax.ShapeDtypeStruct((t, D_MODEL), F32),
            jax.ShapeDtypeStruct((D_MODEL, t), BF16),
            jax.ShapeDtypeStruct((nhc, N_KEYS, t), F32),
        ],
        compiler_params=pltpu.CompilerParams(
            dimension_semantics=("arbitrary",), vmem_limit_bytes=VMEM_LIMIT),
        name="outproj",
    )(x2, oa, ob, wo_bf, g2, wqt_bf, keys_bf)


def _cand_pairs():
    return [(r, c) for r in range(PEER_TOPK) for c in range(PEER_TOPK) if (r + 1) * (c + 1) <= PEER_TOPK]


def _router_kernel(sc_ref, a_ref, n_ref, b_ref, r_ref, *, tr):
    iota = lax.broadcasted_iota(jnp.int32, (N_KEYS, tr), 0).astype(F32)

    def topk_axis0(s):
        rank = jnp.full((N_KEYS, tr), float(PEER_TOPK), F32)
        vals = []
        work = s
        for r in range(PEER_TOPK):
            m = jnp.max(work, axis=0, keepdims=True)
            idx = jnp.min(jnp.where(work == m, iota, float(N_KEYS)), axis=0, keepdims=True)
            hit = iota == idx
            rank = jnp.where(hit, float(r), rank)
            work = jnp.where(hit, -jnp.inf, work)
            vals.append(m)
        return rank, vals

    ranks1, ranks2, vals1, vals2 = [], [], [], []
    for h in range(PEER_HEADS):
        rk, vl = topk_axis0(sc_ref[2 * h])
        ranks1.append(rk)
        vals1.append(vl)
        rk, vl = topk_axis0(sc_ref[2 * h + 1])
        ranks2.append(rk)
        vals2.append(vl)

    v1 = [jnp.concatenate([vals1[h][r] for h in range(PEER_HEADS)], axis=0) for r in range(PEER_TOPK)]
    v2 = [jnp.concatenate([vals2[h][r] for h in range(PEER_HEADS)], axis=0) for r in range(PEER_TOPK)]
    pairs = _cand_pairs()
    sums = {rc: v1[rc[0]] + v2[rc[1]] for rc in pairs}
    cnt = {rc: jnp.full((PEER_HEADS, tr), float((rc[0] + 1) * (rc[1] + 1) - 1), F32) for rc in pairs}
    for ia, x in enumerate(pairs):
        for y in pairs[ia + 1:]:
            if y[0] >= x[0] and y[1] >= x[1]:
                continue
            if x[0] >= y[0] and x[1] >= y[1]:
                continue
            g = jnp.where(sums[y] > sums[x], 1.0, 0.0)
            cnt[x] = cnt[x] + g
            cnt[y] = cnt[y] + (1.0 - g)
    e1 = [jnp.exp(v1[r] - v1[0]) for r in range(PEER_TOPK)]
    e2 = [jnp.exp(v2[c] - v2[0]) for c in range(PEER_TOPK)]
    nsel = [jnp.zeros((PEER_HEADS, tr), F32) for _ in range(PEER_TOPK)]
    zsum = jnp.zeros((PEER_HEADS, tr), F32)
    for rc in pairs:
        sel = jnp.where(cnt[rc] < float(PEER_TOPK), 1.0, 0.0)
        nsel[rc[0]] = nsel[rc[0]] + sel
        zsum = zsum + sel * (e1[rc[0]] * e2[rc[1]])
    zinv = 1.0 / zsum

    for h in range(PEER_HEADS):
        s1 = sc_ref[2 * h]
        s2 = sc_ref[2 * h + 1]
        nd = jnp.zeros((N_KEYS, tr), F32)
        for r in range(PEER_TOPK):
            nd = jnp.where(ranks1[h] == float(r), nsel[r][h:h + 1, :], nd)
        n_ref[h] = nd
        a_ref[h] = jnp.exp(s1 - vals1[h][0])
        b_ref[h] = jnp.exp(s2 - vals2[h][0]) * zinv[h:h + 1, :]
        r_ref[h] = ranks2[h]


def _router(sct):
    t = sct.shape[-1]
    tr = _pick(t, (256, 128))
    kern = functools.partial(_router_kernel, tr=tr)
    shp = jax.ShapeDtypeStruct((PEER_HEADS, N_KEYS, t), F32)
    spec = pl.BlockSpec((PEER_HEADS, N_KEYS, tr), lambda i: (0, 0, i))
    return pl.pallas_call(
        kern,
        grid=(t // tr,),
        in_specs=[pl.BlockSpec((2 * PEER_HEADS, N_KEYS, tr), lambda i: (0, 0, i))],
        out_specs=[spec, spec, spec, spec],
        out_shape=[shp, shp, shp, shp],
        compiler_params=pltpu.CompilerParams(
            dimension_semantics=("arbitrary",), vmem_limit_bytes=VMEM_LIMIT),
        name="router",
    )(sct)


def _experts_kernel(xnt_ref, u_ref, vt_ref, a_ref, n_ref, b_ref, r_ref, h1_ref, o_ref,
                    acc_ref, act_ref, *, eb):
    e = pl.program_id(1)

    @pl.when(e == 0)
    def _():
        acc_ref[...] = jnp.zeros_like(acc_ref)

    hid = jnp.dot(u_ref[...], xnt_ref[...], preferred_element_type=F32)
    na = eb // N_KEYS
    for al in range(na):
        a_idx = e * na + al
        w = None
        for h in range(PEER_HEADS):
            n_row = n_ref[h, pl.ds(a_idx, 1), :]
            a_row = a_ref[h, pl.ds(a_idx, 1), :]
            term = jnp.where(r_ref[h] < n_row, a_row * b_ref[h], 0.0)
            w = term if w is None else w + term
        hh = hid[al * N_KEYS:(al + 1) * N_KEYS, :]
        gelu = 0.5 * hh * (1.0 + lax.erf(hh * (2.0 ** -0.5)))
        act_ref[al * N_KEYS:(al + 1) * N_KEYS, :] = (gelu * w).astype(BF16)
    acc_ref[...] += jnp.dot(vt_ref[...], act_ref[...], preferred_element_type=F32)

    @pl.when(e == pl.num_programs(1) - 1)
    def _():
        o_ref[...] = h1_ref[...] + acc_ref[...].T


def _experts(xnt, u_bf, vt_bf, ga, gn, gb, gr, h1):
    t = h1.shape[0]
    tb = _pick(t, (512, 256, 128))
    eb = 512
    kern = functools.partial(_experts_kernel, eb=eb)
    gspec = pl.BlockSpec((PEER_HEADS, N_KEYS, tb), lambda i, e: (0, 0, i))
    return pl.pallas_call(
        kern,
        grid=(t // tb, N_EXPERTS // eb),
        in_specs=[
            pl.BlockSpec((D_MODEL, tb), lambda i, e: (0, i)),
            pl.BlockSpec((eb, D_MODEL), lambda i, e: (e, 0)),
            pl.BlockSpec((D_MODEL, eb), lambda i, e: (0, e)),
            gspec, gspec, gspec, gspec,
            pl.BlockSpec((tb, D_MODEL), lambda i, e: (i, 0)),
        ],
        out_specs=pl.BlockSpec((tb, D_MODEL), lambda i, e: (i, 0)),
        out_shape=jax.ShapeDtypeStruct((t, D_MODEL), F32),
        scratch_shapes=[pltpu.VMEM((D_MODEL, tb), F32), pltpu.VMEM((eb, tb), BF16)],
        compiler_params=pltpu.CompilerParams(
            dimension_semantics=("arbitrary", "arbitrary"), vmem_limit_bytes=VMEM_LIMIT),
        name="experts",
    )(xnt, u_bf, vt_bf, ga, gn, gb, gr, h1)


def _rope_tables(seq):
    half = ROPE_DIM // 2
    pos = jnp.arange(seq, dtype=F32)
    inv_freq = 1.0 / (ROPE_THETA ** (jnp.arange(0, ROPE_DIM, 2, dtype=F32) / ROPE_DIM))
    ang = pos[:, None] * inv_freq[None, :]
    cos, sin = jnp.cos(ang), jnp.sin(ang)
    one = jnp.ones((seq, HEAD_DIM - ROPE_DIM), F32)
    zero_h = jnp.zeros((seq, half), F32)
    zero_r = jnp.zeros((seq, HEAD_DIM - ROPE_DIM), F32)
    cos_h = jnp.concatenate([cos, cos, one], axis=-1)
    sa_h = jnp.concatenate([-sin, zero_h, zero_r], axis=-1)
    sb_h = jnp.concatenate([zero_h, sin, zero_r], axis=-1)
    rep = LANES // HEAD_DIM
    return (jnp.tile(cos_h, (1, rep)), jnp.tile(sa_h, (1, rep)), jnp.tile(sb_h, (1, rep)))


def _layer(h_res, l, norm1_g, w_in, da_qk_norm_g, da_lambda, da_subln_g, w_out, norm2_g,
           peer_wq, peer_keys, peer_u, peer_v, batch, seq, tables):
    lambda_init = 0.8 - 0.6 * math.exp(-0.3 * l)
    cosf, saf, sbf = tables
    gqk = jnp.tile(da_qk_norm_g[l].astype(F32), (1, HALF_WIDTH // HEAD_DIM))
    grp = jnp.arange(HALF_WIDTH) // HEAD_DIM
    gsum = (grp[:, None] == grp[None, :]).astype(BF16)
    proj = _inproj(h_res, norm1_g[l][None, :].astype(F32), w_in[l].astype(BF16), gqk, gsum,
                   cosf, saf, sbf, seq)
    oa = _diffattn(proj, da_lambda[l].astype(F32), da_subln_g[l][None, :].astype(F32),
                   batch, seq, lambda_init)
    ob = _stickbreak(proj, batch, seq)
    keys_bf = peer_keys[l].reshape(2 * PEER_HEADS, N_KEYS, N_KEYS).astype(BF16)
    h1, xnt, sct = _outproj(h_res, oa, ob, w_out[l].astype(BF16), norm2_g[l][None, :].astype(F32),
                            peer_wq[l].T.astype(BF16), keys_bf)
    ga, gn, gb, gr = _router(sct)
    return _experts(xnt, peer_u[l].astype(BF16), peer_v[l].T.astype(BF16), ga, gn, gb, gr, h1)


def kernel(x, norm1_g, w_in, da_qk_norm_g, da_lambda, da_subln_g, w_out, norm2_g,
           peer_wq, peer_keys, peer_u, peer_v):
    batch, seq, d = x.shape
    assert d == D_MODEL and seq % LANES == 0
    tables = _rope_tables(seq)
    h = x.astype(F32).reshape(batch * seq, d)
    for l in range(norm1_g.shape[0]):
        h = _layer(h, l, norm1_g, w_in, da_qk_norm_g, da_lambda, da_subln_g, w_out, norm2_g,
                   peer_wq, peer_keys, peer_u, peer_v, batch, seq, tables)
    return h.reshape(batch, seq, d).astype(x.dtype)
```
